```python
import jax, jax.numpy as jnp
from jax import lax
import numpy as np

D_MODEL = 2048
BATCH = 8
SEQ = 2048
DEPTH = 2
DEC_BATCH = 32
DEC_SEQ = 4
PAST_LEN = 8192
PAGE_SIZE = 128

ATT_WIDTH = D_MODEL // 2
POOL_WIDTH = D_MODEL - ATT_WIDTH
HEAD_DIM = 128
N_HEADS = ATT_WIDTH // HEAD_DIM
MOBA_BLOCK = 256
MOBA_TOPK = 3
Q_CHUNK = 16
POOL_WINDOWS = (2, 4, 8, 16)
N_POOL_GROUPS = len(POOL_WINDOWS)
POOL_GROUP = POOL_WIDTH // N_POOL_GROUPS
POOL_STATE = max(POOL_WINDOWS) - 1
D_IN = 3 * ATT_WIDTH + POOL_WIDTH
D_FF = 4 * D_MODEL
D_PLE = 256
EPS = 1e-6

kernel_name = "hymba_moba_pool_decoder_step"


def rmsnorm(x, g):
    xf = x.astype(jnp.float32)
    y = xf * lax.rsqrt(jnp.mean(xf * xf, axis=-1, keepdims=True) + EPS)
    return (y * g.astype(jnp.float32)).astype(x.dtype)


def moba_attention(q, k, v, q_pos):
    B, Sq, H, Dh = q.shape
    L = k.shape[1]
    nb = -(-L // MOBA_BLOCK)
    pad = nb * MOBA_BLOCK - L
    k = jnp.pad(k, ((0, 0), (0, pad), (0, 0), (0, 0)))
    v = jnp.pad(v, ((0, 0), (0, pad), (0, 0), (0, 0)))
    kbh = k.reshape(B, nb, MOBA_BLOCK, H, Dh).transpose(0, 3, 1, 2, 4)
    vbh = v.reshape(B, nb, MOBA_BLOCK, H, Dh).transpose(0, 3, 1, 2, 4)
    kmean = jnp.mean(kbh.astype(jnp.float32), axis=3)
    k_sel = min(MOBA_TOPK, nb)
    C = min(Q_CHUNK, Sq)
    n_c = -(-Sq // C)
    qpad = n_c * C - Sq
    qp = jnp.pad(q, ((0, 0), (0, qpad), (0, 0), (0, 0)))
    pp = jnp.pad(q_pos, (0, qpad), mode='edge')
    q_chunks = qp.reshape(B, n_c, C, H, Dh).transpose(1, 0, 2, 3, 4)
    p_chunks = pp.reshape(n_c, C)
    bi = jnp.arange(B)[:, None, None, None]
    hi = jnp.arange(H)[None, :, None, None]
    blk_ids = jnp.arange(nb, dtype=jnp.int32)
    offs = jnp.arange(MOBA_BLOCK, dtype=jnp.int32)
    scale = HEAD_DIM ** -0.5

    def one_chunk(args):
        qc, pc = args
        own = pc // MOBA_BLOCK
        s_blk = jnp.einsum('bchd,bhnd->bhcn', qc.astype(jnp.float32), kmean)
        past = blk_ids[None, :] < own[:, None]
        s_blk = jnp.where(past, s_blk, -jnp.inf)
        _, top_i = lax.top_k(s_blk, k_sel)
        own_b = jnp.broadcast_to(own[None, None, :, None], (B, H, C, 1)).astype(top_i.dtype)
        idx = jnp.concatenate([top_i, own_b], axis=-1)
        valid = jnp.concatenate([top_i < own_b, jnp.ones(own_b.shape, dtype=bool)], axis=-1)
        kg = kbh[bi, hi, idx]
        vg = vbh[bi, hi, idx]
        s = jnp.einsum('bchd,bhcnkd->bhcnk', qc, kg).astype(jnp.float32) * scale
        kpos = idx[..., None] * MOBA_BLOCK + offs
        mask = valid[..., None] & (kpos <= pc[None, None, :, None, None])
        s = jnp.where(mask, s, -jnp.inf)
        p = jax.nn.softmax(s.reshape(B, H, C, -1), axis=-1).reshape(s.shape)
        return jnp.einsum('bhcnk,bhcnkd->bchd', p.astype(vg.dtype), vg)

    out = lax.map(one_chunk, (q_chunks, p_chunks))
    return out.transpose(1, 0, 2, 3, 4).reshape(B, n_c * C, H, Dh)[:, :Sq]


def pool_mixer(u, prev, pos0, w_pool, pool_scale):
    B, S, _ = u.shape
    P = POOL_STATE
    xp = jnp.concatenate([prev.astype(u.dtype), u], axis=1)
    c = jnp.cumsum(xp.astype(jnp.float32), axis=1)
    c = jnp.concatenate([jnp.zeros((B, 1, POOL_WIDTH), jnp.float32), c], axis=1)
    t = jnp.arange(S)
    upper = c[:, P + 1:P + 1 + S]
    outs = []
    for g, w in enumerate(POOL_WINDOWS):
        sl = slice(g * POOL_GROUP, (g + 1) * POOL_GROUP)
        lower = c[:, P + 1 - w:P + 1 - w + S, sl]
        cnt = jnp.minimum(w, pos0 + t + 1).astype(jnp.float32)
        mean = (upper[..., sl] - lower) / cnt[None, :, None]
        outs.append(mean - u[..., sl].astype(jnp.float32))
    d = jnp.stack(outs, axis=2)
    y = jnp.einsum('bsgc,gcd->bsgd', d.astype(u.dtype), w_pool).reshape(B, S, POOL_WIDTH)
    return y * pool_scale, xp[:, -P:]


def decoder_layer(h, p_l, past_k, past_v, prev_pool, pos0, g_mix, w_in, w_pool, pool_scale,
                  w_out, g_mlp, w_up, w_down, g_ple, w_gate, b_gate, w_ple):
    B, S, _ = h.shape
    a = rmsnorm(h, g_mix)
    z = a @ w_in
    q = z[..., :ATT_WIDTH].reshape(B, S, N_HEADS, HEAD_DIM)
    k = z[..., ATT_WIDTH:2 * ATT_WIDTH].reshape(B, S, N_HEADS, HEAD_DIM)
    v = z[..., 2 * ATT_WIDTH:3 * ATT_WIDTH].reshape(B, S, N_HEADS, HEAD_DIM)
    u = z[..., 3 * ATT_WIDTH:]
    q_pos = pos0 + jnp.arange(S, dtype=jnp.int32)
    if past_k is None:
        k_all, v_all = k, v
    else:
        k_all = jnp.concatenate([past_k.astype(k.dtype), k], axis=1)
        v_all = jnp.concatenate([past_v.astype(v.dtype), v], axis=1)
    att = moba_attention(q, k_all, v_all, q_pos)
    pool_out, pool_state = pool_mixer(u, prev_pool, pos0, w_pool, pool_scale)
    mix = jnp.concatenate([att.reshape(B, S, ATT_WIDTH), pool_out.astype(att.dtype)], axis=-1) @ w_out
    h = h + mix
    f = rmsnorm(h, g_mlp) @ w_up
    h = h + jnp.square(jax.nn.relu(f)) @ w_down
    gate = jax.nn.sigmoid((rmsnorm(h, g_ple) @ w_gate + b_gate).astype(jnp.float32))
    h = h + ((p_l @ w_ple).astype(jnp.float32) * gate).astype(h.dtype)
    return h, k, v, pool_state


def setup_inputs(seed: int = 0) -> dict:
    key = jax.random.key(seed)
    ks = jax.random.split(key, 24)
    f32 = jnp.float32
    n_pages = PAST_LEN // PAGE_SIZE
    n_used = DEC_BATCH * n_pages
    n_pool = n_used + max(1, n_used // 4)
    nrm = lambda k, s, sc: jax.random.normal(k, s, f32) * sc
    page_table = jax.random.permutation(ks[0], n_pool)[:n_used].reshape(DEC_BATCH, n_pages).astype(jnp.int32)
    return {
        "x_prompt": nrm(ks[1], (BATCH, SEQ, D_MODEL), 1.0),
        "x_sample": nrm(ks[2], (DEC_BATCH, DEC_SEQ, D_MODEL), 1.0),
        "p_prompt": nrm(ks[3], (DEPTH, BATCH, SEQ, D_PLE), 1.0),
        "p_sample": nrm(ks[4], (DEPTH, DEC_BATCH, DEC_SEQ, D_PLE), 1.0),
        "cache_k": nrm(ks[5], (DEPTH, n_pool, PAGE_SIZE, N_HEADS, HEAD_DIM), 1.0),
        "cache_v": nrm(ks[6], (DEPTH, n_pool, PAGE_SIZE, N_HEADS, HEAD_DIM), 1.0),
        "state_pool": nrm(ks[7], (DEPTH, DEC_BATCH, POOL_STATE, POOL_WIDTH), 1.0),
        "page_table": page_table,
        "g_mix": 1.0 + nrm(ks[8], (DEPTH, D_MODEL), 0.02),
        "w_in": nrm(ks[9], (DEPTH, D_MODEL, D_IN), D_MODEL ** -0.5),
        "w_pool": nrm(ks[10], (DEPTH, N_POOL_GROUPS, POOL_GROUP, POOL_GROUP), POOL_GROUP ** -0.5),
        "pool_scale": 1.0 + nrm(ks[11], (DEPTH, POOL_WIDTH), 0.02),
        "w_out": nrm(ks[12], (DEPTH, D_MODEL, D_MODEL), D_MODEL ** -0.5),
        "g_mlp": 1.0 + nrm(ks[13], (DEPTH, D_MODEL), 0.02),
        "w_up": nrm(ks[14], (DEPTH, D_MODEL, D_FF), D_MODEL ** -0.5),
        "w_down": nrm(ks[15], (DEPTH, D_FF, D_MODEL), D_FF ** -0.5),
        "g_ple": 1.0 + nrm(ks[16], (DEPTH, D_MODEL), 0.02),
        "w_gate": nrm(ks[17], (DEPTH, D_MODEL, D_MODEL), D_MODEL ** -0.5),
        "b_gate": nrm(ks[18], (DEPTH, D_MODEL), 0.01),
        "w_ple": nrm(ks[19], (DEPTH, D_PLE, D_MODEL), D_PLE ** -0.5),
        "g_final": 1.0 + nrm(ks[20], (D_MODEL,), 0.02),
    }


def reference(x_prompt, x_sample, p_prompt, p_sample, cache_k, cache_v, state_pool, page_table,
              g_mix, w_in, w_pool, pool_scale, w_out, g_mlp, w_up, w_down, g_ple, w_gate,
              b_gate, w_ple, g_final):
    Bp, Sp, _ = x_prompt.shape
    Bs = x_sample.shape[0]
    past_len = page_table.shape[1] * PAGE_SIZE
    hp, hs = x_prompt, x_sample
    prev_prompt = jnp.zeros((Bp, POOL_STATE, POOL_WIDTH), x_prompt.dtype)
    kp, vp, sp, ksm, vsm, ssm = [], [], [], [], [], []
    for l in range(DEPTH):
        hp, k_, v_, s_ = decoder_layer(
            hp, p_prompt[l], None, None, prev_prompt, 0,
            g_mix[l], w_in[l], w_pool[l], pool_scale[l], w_out[l], g_mlp[l], w_up[l], w_down[l],
            g_ple[l], w_gate[l], b_gate[l], w_ple[l])
        kp.append(k_.reshape(Bp, Sp // PAGE_SIZE, PAGE_SIZE, N_HEADS, HEAD_DIM))
        vp.append(v_.reshape(Bp, Sp // PAGE_SIZE, PAGE_SIZE, N_HEADS, HEAD_DIM))
        sp.append(s_)
        past_k = cache_k[l, page_table].reshape(Bs, past_len, N_HEADS, HEAD_DIM)
        past_v = cache_v[l, page_table].reshape(Bs, past_len, N_HEADS, HEAD_DIM)
        hs, k_, v_, s_ = decoder_layer(
            hs, p_sample[l], past_k, past_v, state_pool[l], past_len,
            g_mix[l], w_in[l], w_pool[l], pool_scale[l], w_out[l], g_mlp[l], w_up[l], w_down[l],
            g_ple[l], w_gate[l], b_gate[l], w_ple[l])
        ksm.append(k_)
        vsm.append(v_)
        ssm.append(s_)
    y_prompt = rmsnorm(hp, g_final)
    y_sample = rmsnorm(hs, g_final)
    return (y_prompt, y_sample, jnp.stack(kp), jnp.stack(vp), jnp.stack(sp),
            jnp.stack(ksm), jnp.stack(vsm), jnp.stack(ssm))
```

```python
import functools

import jax
import jax.numpy as jnp
from jax import lax
from jax.experimental import pallas as pl
from jax.experimental.pallas import tpu as pltpu

D_MODEL = 2048
ATT_WIDTH = 1024
POOL_WIDTH = 1024
HEAD_DIM = 128
N_HEADS = 8
MOBA_BLOCK = 256
MOBA_TOPK = 3
PAGE_SIZE = 128
POOL_WINDOWS = (2, 4, 8, 16)
POOL_GROUP = 256
POOL_STATE = 15
POOL_HALO = 16
D_IN = 4096
D_FF = 8192
D_PLE = 256
EPS = 1e-6
SCALE = HEAD_DIM ** -0.5
PAGES_PER_BLOCK = MOBA_BLOCK // PAGE_SIZE

F32 = jnp.float32
BF16 = jnp.bfloat16
NEG_INF = float("-inf")
MIB = 1024 * 1024

_NT = (((1,), (1,)), ((), ()))
_TN = (((0,), (0,)), ((), ()))


def _rmsnorm(x, g):
    y = x * lax.rsqrt(jnp.mean(x * x, axis=-1, keepdims=True) + EPS)
    return y * g


def _params(semantics, vmem_mib):
    return pltpu.CompilerParams(dimension_semantics=semantics, vmem_limit_bytes=vmem_mib * MIB)


def _in_proj_kernel(h_ref, g_ref, w_ref, z_ref, a_ref):
    @pl.when(pl.program_id(1) == 0)
    def _():
        a_ref[...] = _rmsnorm(h_ref[...], g_ref[...]).astype(BF16)

    z_ref[...] = jnp.dot(a_ref[...], w_ref[...], preferred_element_type=F32)


def _in_proj(h, g, w, tm, tn):
    T, D = h.shape
    N = w.shape[1]
    return pl.pallas_call(
        _in_proj_kernel,
        grid=(T // tm, N // tn),
        in_specs=[
            pl.BlockSpec((tm, D), lambda i, j: (i, 0)),
            pl.BlockSpec((1, D), lambda i, j: (0, 0)),
            pl.BlockSpec((D, tn), lambda i, j: (0, j)),
        ],
        out_specs=pl.BlockSpec((tm, tn), lambda i, j: (i, j)),
        out_shape=jax.ShapeDtypeStruct((T, N), F32),
        scratch_shapes=[pltpu.VMEM((tm, D), BF16)],
        compiler_params=_params(("parallel", "arbitrary"), 48),
        name="in_proj",
    )(h, g.reshape(1, D), w)


def _attn_prompt_kernel(q_ref, k_ref, v_ref, e_ref, o_ref, kb_ref, vb_ref, km_ref, *, nb):
    nbp = km_ref.shape[0]
    kb_ref[...] = k_ref[...].astype(BF16)
    vb_ref[...] = v_ref[...].astype(BF16)
    km_ref[...] = jnp.zeros(km_ref.shape, F32)
    for n in range(nb):
        blk_sum = jnp.sum(k_ref[n * MOBA_BLOCK:(n + 1) * MOBA_BLOCK, :], axis=0, keepdims=True)
        km_ref[n:n + 1, :] = blk_sum * (1.0 / MOBA_BLOCK)

    row = lax.broadcasted_iota(jnp.int32, (MOBA_BLOCK, MOBA_BLOCK), 0)
    col = lax.broadcasted_iota(jnp.int32, (MOBA_BLOCK, MOBA_BLOCK), 1)
    causal = col <= row
    blk = lax.broadcasted_iota(jnp.int32, (nbp, MOBA_BLOCK), 0)

    for qi in range(nb):
        lo, hi = qi * MOBA_BLOCK, (qi + 1) * MOBA_BLOCK
        q = q_ref[lo:hi, :]
        qb = q.astype(BF16)
        s = lax.dot_general(qb, kb_ref[0:hi, :], _NT, preferred_element_type=F32) * SCALE
        s_diag = jnp.where(causal, s[:, lo:hi], NEG_INF)
        m = jnp.max(s_diag, axis=1, keepdims=True)
        if qi > 0:
            s_past = s[:, 0:lo]
            if qi > MOBA_TOPK:
                sb = lax.dot_general(km_ref[...], q, _NT, precision=lax.Precision.HIGHEST,
                                     preferred_element_type=F32)
                past = blk < qi
                sb = jnp.where(past, sb, NEG_INF)
                rank = jnp.zeros((nbp, MOBA_BLOCK), F32)
                for mth in range(qi):
                    r_m = sb[mth:mth + 1, :]
                    ahead = (r_m > sb) | ((r_m == sb) & (blk > mth))
                    rank = rank + jnp.where(ahead, 1.0, 0.0)
                sel = jnp.where(past & (rank < MOBA_TOPK), 1.0, 0.0).astype(BF16)
                allow = lax.dot_general(sel, e_ref[:, 0:lo], _TN, preferred_element_type=F32)
                s_past = jnp.where(allow > 0.5, s_past, NEG_INF)
            m = jnp.maximum(m, jnp.max(s_past, axis=1, keepdims=True))
        p_diag = jnp.exp(s_diag - m)
        l = jnp.sum(p_diag, axis=1, keepdims=True)
        o = jnp.dot(p_diag.astype(BF16), vb_ref[lo:hi, :], preferred_element_type=F32)
        if qi > 0:
            p_past = jnp.exp(s_past - m)
            l = l + jnp.sum(p_past, axis=1, keepdims=True)
            o = o + jnp.dot(p_past.astype(BF16), vb_ref[0:lo, :], preferred_element_type=F32)
        o_ref[lo:hi, :] = (o * (1.0 / l)).astype(o_ref.dtype)


def _block_indicator(nbp, n_keys):
    n = lax.broadcasted_iota(jnp.int32, (nbp, n_keys), 0)
    j = lax.broadcasted_iota(jnp.int32, (nbp, n_keys), 1)
    return (j // MOBA_BLOCK == n).astype(BF16)


def _attn_prompt(z3):
    B, S, _ = z3.shape
    assert S % MOBA_BLOCK == 0
    nb = S // MOBA_BLOCK
    nbp = -(-nb // 8) * 8
    e = _block_indicator(nbp, S)
    spec = lambda off: pl.BlockSpec((None, S, HEAD_DIM), lambda b, h: (b, 0, off + h))
    return pl.pallas_call(
        functools.partial(_attn_prompt_kernel, nb=nb),
        grid=(B, N_HEADS),
        in_specs=[spec(0), spec(N_HEADS), spec(2 * N_HEADS),
                  pl.BlockSpec((nbp, S), lambda b, h: (0, 0))],
        out_specs=pl.BlockSpec((None, S, HEAD_DIM), lambda b, h: (b, 0, h)),
        out_shape=jax.ShapeDtypeStruct((B, S, ATT_WIDTH), BF16),
        scratch_shapes=[pltpu.VMEM((S, HEAD_DIM), BF16), pltpu.VMEM((S, HEAD_DIM), BF16),
                        pltpu.VMEM((nbp, HEAD_DIM), F32)],
        compiler_params=_params(("parallel", "parallel"), 48),
        name="attn_prompt",
    )(z3, z3, z3, e)


def _pool_prompt_kernel(u_ref, halo_ref, w_ref, sc_ref, o_ref, xp_ref, *, ts):
    i = pl.program_id(1)

    @pl.when(i == 0)
    def _():
        xp_ref[0:POOL_HALO, :] = jnp.zeros((POOL_HALO, POOL_WIDTH), F32)

    @pl.when(i > 0)
    def _():
        xp_ref[0:POOL_HALO, :] = halo_ref[...]

    xp_ref[POOL_HALO:POOL_HALO + ts, :] = u_ref[...]
    pos = i * ts + lax.broadcasted_iota(jnp.int32, (ts, 1), 0)
    for g, w in enumerate(POOL_WINDOWS):
        c0, c1 = g * POOL_GROUP, (g + 1) * POOL_GROUP
        x = xp_ref[:, c0:c1]
        acc = x
        shift = 1
        while shift < w:
            acc = acc + pltpu.roll(acc, shift, axis=0)
            shift *= 2
        cnt = jnp.minimum(w, pos + 1).astype(F32)
        d = acc[POOL_HALO:, :] * (1.0 / cnt) - x[POOL_HALO:, :]
        y = jnp.dot(d.astype(BF16), w_ref[g], preferred_element_type=F32) * sc_ref[:, c0:c1]
        o_ref[:, c0:c1] = y.astype(o_ref.dtype)


def _pool_prompt(z3, w_pool, pool_scale, ts):
    B, S, _ = z3.shape
    u_blk = (D_IN - POOL_WIDTH) // POOL_WIDTH
    halo_per_tile = ts // POOL_HALO
    return pl.pallas_call(
        functools.partial(_pool_prompt_kernel, ts=ts),
        grid=(B, S // ts),
        in_specs=[
            pl.BlockSpec((None, ts, POOL_WIDTH), lambda b, i: (b, i, u_blk)),
            pl.BlockSpec((None, POOL_HALO, POOL_WIDTH),
                         lambda b, i: (b, jnp.maximum(i * halo_per_tile - 1, 0), u_blk)),
            pl.BlockSpec((len(POOL_WINDOWS), POOL_GROUP, POOL_GROUP), lambda b, i: (0, 0, 0)),
            pl.BlockSpec((1, POOL_WIDTH), lambda b, i: (0, 0)),
        ],
        out_specs=pl.BlockSpec((None, ts, POOL_WIDTH), lambda b, i: (b, i, 0)),
        out_shape=jax.ShapeDtypeStruct((B, S, POOL_WIDTH), BF16),
        scratch_shapes=[pltpu.VMEM((POOL_HALO + ts, POOL_WIDTH), F32)],
        compiler_params=_params(("parallel", "arbitrary"), 48),
        name="pool_prompt",
    )(z3, z3, w_pool, pool_scale.reshape(1, POOL_WIDTH))


def _pool_sample_kernel(prev_ref, u_ref, w_ref, sc_ref, o_ref, *, pos0):
    S, B, _ = u_ref.shape
    for g, w in enumerate(POOL_WINDOWS):
        c0, c1 = g * POOL_GROUP, (g + 1) * POOL_GROUP
        rows = []
        for t in range(S):
            acc = u_ref[t, :, c0:c1]
            for back in range(1, w):
                src = t - back
                acc = acc + (u_ref[src, :, c0:c1] if src >= 0 else prev_ref[POOL_STATE + src, :, c0:c1])
            cnt = float(min(w, pos0 + t + 1))
            rows.append(acc * (1.0 / cnt) - u_ref[t, :, c0:c1])
        d = jnp.concatenate(rows, axis=0)
        y = jnp.dot(d.astype(BF16), w_ref[g], preferred_element_type=F32) * sc_ref[:, c0:c1]
        for t in range(S):
            o_ref[t, :, c0:c1] = y[t * B:(t + 1) * B, :].astype(o_ref.dtype)


def _pool_sample(u_t, prev_t, w_pool, pool_scale, pos0):
    S, B, _ = u_t.shape
    assert pos0 >= POOL_STATE and S <= POOL_STATE + 1
    return pl.pallas_call(
        functools.partial(_pool_sample_kernel, pos0=pos0),
        out_shape=jax.ShapeDtypeStruct((S, B, POOL_WIDTH), BF16),
        name="pool_sample",
    )(prev_t, u_t, w_pool, pool_scale.reshape(1, POOL_WIDTH))


def _out_proj_kernel(att_ref, pool_ref, wa_ref, wp_ref, h_ref, o_ref):
    o_ref[...] = (h_ref[...]
                  + jnp.dot(att_ref[...], wa_ref[...], preferred_element_type=F32)
                  + jnp.dot(pool_ref[...], wp_ref[...], preferred_element_type=F32))


def _out_proj(att, pool, w_out, h, tm, tn):
    T, D = h.shape
    return pl.pallas_call(
        _out_proj_kernel,
        grid=(T // tm, D // tn),
        in_specs=[
            pl.BlockSpec((tm, ATT_WIDTH), lambda i, j: (i, 0)),
            pl.BlockSpec((tm, POOL_WIDTH), lambda i, j: (i, 0)),
            pl.BlockSpec((ATT_WIDTH, tn), lambda i, j: (0, j)),
            pl.BlockSpec((POOL_WIDTH, tn), lambda i, j: (1, j)),
            pl.BlockSpec((tm, tn), lambda i, j: (i, j)),
        ],
        out_specs=pl.BlockSpec((tm, tn), lambda i, j: (i, j)),
        out_shape=jax.ShapeDtypeStruct((T, D), F32),
        compiler_params=_params(("parallel", "parallel"), 48),
        name="out_proj",
    )(att, pool, w_out, w_out, h)


def _mlp_kernel(h_ref, g_ref, wu_ref, wd_ref, o_ref, a_ref):
    @pl.when(pl.program_id(1) == 0)
    def _():
        h = h_ref[...]
        a_ref[...] = _rmsnorm(h, g_ref[...]).astype(BF16)
        o_ref[...] = h

    f = jnp.dot(a_ref[...], wu_ref[...], preferred_element_type=F32)
    r = jnp.maximum(f, 0.0)
    o_ref[...] += jnp.dot((r * r).astype(BF16), wd_ref[...], preferred_element_type=F32)


def _mlp(h, g, w_up, w_down, tm, tf):
    T, D = h.shape
    return pl.pallas_call(
        _mlp_kernel,
        grid=(T // tm, D_FF // tf),
        in_specs=[
            pl.BlockSpec((tm, D), lambda i, f: (i, 0)),
            pl.BlockSpec((1, D), lambda i, f: (0, 0)),
            pl.BlockSpec((D, tf), lambda i, f: (0, f)),
            pl.BlockSpec((tf, D), lambda i, f: (f, 0)),
        ],
        out_specs=pl.BlockSpec((tm, D), lambda i, f: (i, 0)),
        out_shape=jax.ShapeDtypeStruct((T, D), F32),
        scratch_shapes=[pltpu.VMEM((tm, D), BF16)],
        compiler_params=_params(("parallel", "arbitrary"), 56),
        name="mlp",
    )(h, g.reshape(1, D), w_up, w_down)


def _ple_kernel(h_ref, p_ref, g_ref, wg_ref, bg_ref, wp_ref, gf_ref, o_ref, *, final_norm):
    h = h_ref[...]
    a = _rmsnorm(h, g_ref[...]).astype(BF16)
    gate = jax.nn.sigmoid(jnp.dot(a, wg_ref[...], preferred_element_type=F32) + bg_ref[...])
    emb = jnp.dot(p_ref[...].astype(BF16), wp_ref[...], preferred_element_type=F32)
    out = h + emb * gate
    if final_norm:
        out = _rmsnorm(out, gf_ref[...])
    o_ref[...] = out


def _ple(h, p, g, w_gate, b_gate, w_ple, g_final, tm, final_norm):
    T, D = h.shape
    const = lambda shape: pl.BlockSpec(shape, lambda i: (0, 0))
    return pl.pallas_call(
        functools.partial(_ple_kernel, final_norm=final_norm),
        grid=(T // tm,),
        in_specs=[
            pl.BlockSpec((tm, D), lambda i: (i, 0)),
            pl.BlockSpec((tm, D_PLE), lambda i: (i, 0)),
            const((1, D)), const((D, D)), const((1, D)), const((D_PLE, D)), const((1, D)),
        ],
        out_specs=pl.BlockSpec((tm, D), lambda i: (i, 0)),
        out_shape=jax.ShapeDtypeStruct((T, D), F32),
        compiler_params=_params(("parallel",), 56),
        name="ple",
    )(h, p, g.reshape(1, D), w_gate, b_gate.reshape(1, D), w_ple, g_final.reshape(1, D))


def _attn_sample_kernel(pt_ref, q_ref, kn_ref, vn_ref, *refs, pps, n_pages, n_dec):
    del pt_ref
    k_refs, v_refs = refs[:pps], refs[pps:2 * pps]
    o_ref = refs[2 * pps]
    qb_ref, km_ref, s_ref, p_ref, acc_ref, l_ref = refs[2 * pps + 1:]
    j = pl.program_id(1)
    n_steps = n_pages // pps
    n_blk = n_pages // PAGES_PER_BLOCK
    n_rows = n_dec * N_HEADS
    page_lanes = PAGE_SIZE * N_HEADS

    @pl.when(j == 0)
    def _():
        qb_ref[...] = q_ref[...].astype(BF16)

    @pl.when(j < n_steps)
    def _():
        for i in range(0, pps, PAGES_PER_BLOCK):
            blk_sum = jnp.zeros((N_HEADS, HEAD_DIM), F32)
            for ii in range(i, i + PAGES_PER_BLOCK):
                kp = k_refs[ii][...]
                s_ref[j * pps + ii] = lax.dot_general(qb_ref[...], kp.astype(BF16), _NT,
                                                      preferred_element_type=F32)
                blk_sum = blk_sum + jnp.sum(kp.reshape(PAGE_SIZE, N_HEADS, HEAD_DIM), axis=0)
            n = (j * pps + i) // PAGES_PER_BLOCK
            km = blk_sum * (1.0 / MOBA_BLOCK)
            for h in range(N_HEADS):
                km_ref[h, pl.ds(n, 1), :] = km[h:h + 1, :]

    @pl.when(j == n_steps - 1)
    def _():
        q = q_ref[...]
        row_head = lax.broadcasted_iota(jnp.int32, (n_rows, n_blk), 0) % N_HEADS
        sb = jnp.zeros((n_rows, n_blk), F32)
        for h in range(N_HEADS):
            sb_h = lax.dot_general(q, km_ref[h], _NT, precision=lax.Precision.HIGHEST,
                                   preferred_element_type=F32)
            sb = jnp.where(row_head == h, sb_h, sb)
        nidx = lax.broadcasted_iota(jnp.int32, (n_rows, n_blk), 1)
        rank = jnp.zeros((n_rows, n_blk), F32)
        for mth in range(n_blk):
            c_m = sb[:, mth:mth + 1]
            ahead = (c_m > sb) | ((c_m == sb) & (nidx > mth))
            rank = rank + jnp.where(ahead, 1.0, 0.0)
        sel = jnp.where(rank < MOBA_TOPK, 1.0, 0.0)

        r_i = lax.broadcasted_iota(jnp.int32, (n_rows, page_lanes), 0)
        c_i = lax.broadcasted_iota(jnp.int32, (n_rows, page_lanes), 1)
        own_head = (c_i % N_HEADS) == (r_i % N_HEADS)

        s_new = lax.dot_general(qb_ref[...], kn_ref[...].astype(BF16), _NT, preferred_element_type=F32)
        rn = lax.broadcasted_iota(jnp.int32, s_new.shape, 0)
        cn = lax.broadcasted_iota(jnp.int32, s_new.shape, 1)
        ok_new = ((cn % N_HEADS) == (rn % N_HEADS)) & ((cn // N_HEADS) <= (rn // N_HEADS))
        s_new = jnp.where(ok_new, s_new * SCALE, NEG_INF)
        m_new = jnp.max(s_new, axis=1, keepdims=True)

        m_run = jnp.full((n_rows, page_lanes), NEG_INF, F32)
        for pg in range(n_pages):
            nb_ = pg // PAGES_PER_BLOCK
            keep = own_head & (sel[:, nb_:nb_ + 1] > 0.5)
            sp = jnp.where(keep, s_ref[pg] * SCALE, NEG_INF)
            s_ref[pg] = sp
            m_run = jnp.maximum(m_run, sp)
        m = jnp.maximum(m_new, jnp.max(m_run, axis=1, keepdims=True))

        l_run = jnp.zeros((n_rows, page_lanes), F32)
        for pg in range(n_pages):
            p = jnp.exp(s_ref[pg] - m)
            l_run = l_run + p
            p_ref[pg] = p.astype(BF16)
        p_new = jnp.exp(s_new - m)
        l_ref[...] = jnp.sum(l_run, axis=1, keepdims=True) + jnp.sum(p_new, axis=1, keepdims=True)
        acc_ref[...] = jnp.dot(p_new.astype(BF16), vn_ref[...].astype(BF16), preferred_element_type=F32)

    @pl.when(j >= n_steps)
    def _():
        acc = acc_ref[...]
        for i in range(pps):
            pg = (j - n_steps) * pps + i
            acc = acc + jnp.dot(p_ref[pg], v_refs[i][...].astype(BF16), preferred_element_type=F32)
        acc_ref[...] = acc

    @pl.when(j == 2 * n_steps - 1)
    def _():
        o_ref[...] = (acc_ref[...] * (1.0 / l_ref[...])).astype(o_ref.dtype)


def _attn_sample(q, k_new, v_new, cache_k, cache_v, page_table, layer, pps):
    B, n_rows, _ = q.shape
    n_dec = n_rows // N_HEADS
    n_pages = page_table.shape[1]
    assert n_pages % pps == 0 and pps % PAGES_PER_BLOCK == 0 and n_dec * N_HEADS <= PAGE_SIZE
    n_steps = n_pages // pps
    n_blk = n_pages // PAGES_PER_BLOCK
    page_rows = PAGE_SIZE * N_HEADS

    def k_spec(i):
        return pl.BlockSpec((None, None, page_rows, HEAD_DIM),
                            lambda b, j, pt: (layer, pt[b, jnp.minimum(j, n_steps - 1) * pps + i], 0, 0))

    def v_spec(i):
        return pl.BlockSpec((None, None, page_rows, HEAD_DIM),
                            lambda b, j, pt: (layer, pt[b, jnp.maximum(j - n_steps, 0) * pps + i], 0, 0))

    row_spec = lambda rows: pl.BlockSpec((None, rows, HEAD_DIM), lambda b, j, pt: (b, 0, 0))
    grid_spec = pltpu.PrefetchScalarGridSpec(
        num_scalar_prefetch=1,
        grid=(B, 2 * n_steps),
        in_specs=[row_spec(n_rows), row_spec(PAGE_SIZE), row_spec(PAGE_SIZE)]
        + [k_spec(i) for i in range(pps)] + [v_spec(i) for i in range(pps)],
        out_specs=row_spec(n_rows),
        scratch_shapes=[
            pltpu.VMEM((n_rows, HEAD_DIM), BF16),
            pltpu.VMEM((N_HEADS, n_blk, HEAD_DIM), F32),
            pltpu.VMEM((n_pages, n_rows, page_rows), F32),
            pltpu.VMEM((n_pages, n_rows, page_rows), BF16),
            pltpu.VMEM((n_rows, HEAD_DIM), F32),
            pltpu.VMEM((n_rows, 1), F32),
        ],
    )
    return pl.pallas_call(
        functools.partial(_attn_sample_kernel, pps=pps, n_pages=n_pages, n_dec=n_dec),
        grid_spec=grid_spec,
        out_shape=jax.ShapeDtypeStruct((B, n_rows, HEAD_DIM), BF16),
        compiler_params=_params(("parallel", "arbitrary"), 56),
        name="attn_sample",
    )(page_table, q, k_new, v_new, *([cache_k] * pps), *([cache_v] * pps))


def _tile(total, want):
    return want if total % want == 0 else total


def _layer(h, p, z_consumer, layer, weights, final_norm, tm):
    g_mlp, w_out, w_up, w_down, g_ple, w_gate, b_gate, w_ple, g_final = weights
    att, pool = z_consumer
    h = _out_proj(att, pool, w_out, h, tm, 1024)
    h = _mlp(h, g_mlp, w_up, w_down, tm, 1024)
    return _ple(h, p, g_ple, w_gate, b_gate, w_ple, g_final, tm, final_norm)


def kernel(x_prompt, x_sample, p_prompt, p_sample, cache_k, cache_v, state_pool, page_table,
           g_mix, w_in, w_pool, pool_scale, w_out, g_mlp, w_up, w_down, g_ple, w_gate,
           b_gate, w_ple, g_final):
    depth = w_in.shape[0]
    Bp, Sp, D = x_prompt.shape
    Bs, Ss, _ = x_sample.shape
    n_pages = page_table.shape[1]
    past_len = n_pages * PAGE_SIZE
    assert past_len % MOBA_BLOCK == 0 and Ss <= MOBA_BLOCK
    Tp, Ts = Bp * Sp, Bs * Ss
    n_pool_pages = cache_k.shape[1]

    bf = lambda w: w.astype(BF16)
    w_in_b, w_pool_b, w_out_b, w_up_b, w_down_b, w_gate_b, w_ple_b = map(
        bf, (w_in, w_pool, w_out, w_up, w_down, w_gate, w_ple))
    ck = cache_k.reshape(depth, n_pool_pages, PAGE_SIZE * N_HEADS, HEAD_DIM)
    cv = cache_v.reshape(depth, n_pool_pages, PAGE_SIZE * N_HEADS, HEAD_DIM)

    tm_p = _tile(Tp, 512)
    ts_pool = _tile(Sp, 512)
    tm_s = Ts

    hp = x_prompt.reshape(Tp, D)
    hs = x_sample.reshape(Ts, D)
    kp, vp, sp, ksm, vsm, ssm = [], [], [], [], [], []
    for l in range(depth):
        last = l == depth - 1
        tail_w = (g_mlp[l], w_out_b[l], w_up_b[l], w_down_b[l], g_ple[l], w_gate_b[l], b_gate[l],
                  w_ple_b[l], g_final)

        z = _in_proj(hp, g_mix[l], w_in_b[l], tm_p, 1024)
        z3 = z.reshape(Bp, Sp, D_IN)
        att = _attn_prompt(z3).reshape(Tp, ATT_WIDTH)
        pool = _pool_prompt(z3, w_pool_b[l], pool_scale[l], ts_pool).reshape(Tp, POOL_WIDTH)
        hp = _layer(hp, p_prompt[l].reshape(Tp, D_PLE), (att, pool), l, tail_w, last, tm_p)
        kp.append(z3[:, :, ATT_WIDTH:2 * ATT_WIDTH].reshape(Bp, Sp // PAGE_SIZE, PAGE_SIZE, N_HEADS, HEAD_DIM))
        vp.append(z3[:, :, 2 * ATT_WIDTH:3 * ATT_WIDTH].reshape(Bp, Sp // PAGE_SIZE, PAGE_SIZE, N_HEADS, HEAD_DIM))
        sp.append(z3[:, Sp - POOL_STATE:, 3 * ATT_WIDTH:])

        zs = _in_proj(hs, g_mix[l], w_in_b[l], tm_s, 1024).reshape(Bs, Ss, D_IN)
        q_s = zs[:, :, :ATT_WIDTH].reshape(Bs, Ss * N_HEADS, HEAD_DIM)
        k_s = zs[:, :, ATT_WIDTH:2 * ATT_WIDTH]
        v_s = zs[:, :, 2 * ATT_WIDTH:3 * ATT_WIDTH]
        u_s = zs[:, :, 3 * ATT_WIDTH:]
        pad_rows = lambda a: jnp.pad(a.reshape(Bs, Ss * N_HEADS, HEAD_DIM),
                                     ((0, 0), (0, PAGE_SIZE - Ss * N_HEADS), (0, 0)))
        att_s = _attn_sample(q_s, pad_rows(k_s), pad_rows(v_s), ck, cv, page_table, l, 8)
        att_s = att_s.reshape(Ts, ATT_WIDTH)
        pool_s = _pool_sample(u_s.transpose(1, 0, 2), state_pool[l].transpose(1, 0, 2),
                              w_pool_b[l], pool_scale[l], past_len)
        pool_s = pool_s.transpose(1, 0, 2).reshape(Ts, POOL_WIDTH)
        hs = _layer(hs, p_sample[l].reshape(Ts, D_PLE), (att_s, pool_s), l, tail_w, last, tm_s)
        ksm.append(k_s.reshape(Bs, Ss, N_HEADS, HEAD_DIM))
        vsm.append(v_s.reshape(Bs, Ss, N_HEADS, HEAD_DIM))
        ssm.append(jnp.concatenate([state_pool[l], u_s], axis=1)[:, -POOL_STATE:])

    return (hp.reshape(Bp, Sp, D), hs.reshape(Bs, Ss, D), jnp.stack(kp), jnp.stack(vp), jnp.stack(sp),
            jnp.stack(ksm), jnp.stack(vsm), jnp.stack(ssm))
```
